```python
import math
import jax, jax.numpy as jnp
from jax import lax
import numpy as np

D_MODEL = 4096
BATCH = 2
SEQ = 8192
DEPTH = 2
DEC_BATCH = 4
DEC_SEQ = 4096
PAST_LEN = 128

PLE_DIM = 256
GRID_W = 64
MIX_WIDTH = D_MODEL
POOL_WIDTH = MIX_WIDTH // 4
POOL_WINDOWS = (2, 4, 8, 16)
POOL_GROUP = POOL_WIDTH // len(POOL_WINDOWS)
MLSTM_WIDTH = MIX_WIDTH // 4
MLSTM_HEADS = 4
MLSTM_HEAD_DIM = MLSTM_WIDTH // MLSTM_HEADS
MLSTM_CHUNK = 64
ATTN_WIDTH = MIX_WIDTH - POOL_WIDTH - MLSTM_WIDTH
ATTN_HEAD_DIM = 128
ATTN_HEADS = ATTN_WIDTH // ATTN_HEAD_DIM
ATTN_KV_HEADS = 4
ATTN_GROUP = ATTN_HEADS // ATTN_KV_HEADS
KV_WIDTH = ATTN_KV_HEADS * ATTN_HEAD_DIM
ROPE_AXIS_DIM = ATTN_HEAD_DIM // 2
ROPE_THETA = 10000.0
Q_BLOCK = 128
N_GATES = 4 * MLSTM_HEADS
EPS = 1e-6
IN_SPLITS = (POOL_WIDTH, POOL_WIDTH,
             MLSTM_WIDTH, MLSTM_WIDTH, MLSTM_WIDTH, MLSTM_WIDTH, MLSTM_WIDTH, N_GATES,
             ATTN_WIDTH, KV_WIDTH, KV_WIDTH, ATTN_WIDTH)
IN_WIDTH = sum(IN_SPLITS)

kernel_name = 'hybrid_pool_mlstm_gqa_encoder'


def rmsnorm(x, g):
    xf = x.astype(jnp.float32)
    r = lax.rsqrt(jnp.mean(xf * xf, axis=-1, keepdims=True) + EPS)
    return (xf * r * g.astype(jnp.float32)).astype(x.dtype)


def pool_mixer(u, z, w_pool, scale):
    S = u.shape[1]
    uf = u.astype(jnp.float32)
    cs = jnp.concatenate([jnp.zeros_like(uf[:, :1]), jnp.cumsum(uf, axis=1)], axis=1)
    t = jnp.arange(S)
    outs = []
    for g, w in enumerate(POOL_WINDOWS):
        sl = slice(g * POOL_GROUP, (g + 1) * POOL_GROUP)
        lo = jnp.clip(t - w // 2, 0, S - 1)
        hi = jnp.clip(t + w // 2 - 1, 0, S - 1)
        cnt = (hi - lo + 1).astype(jnp.float32)
        csg = cs[..., sl]
        mean = (jnp.take(csg, hi + 1, axis=1) - jnp.take(csg, lo, axis=1)) / cnt[None, :, None]
        d = (mean - uf[..., sl]).astype(u.dtype)
        outs.append(jnp.einsum('bsc,cd->bsd', d, w_pool[g]))
    y = jnp.concatenate(outs, axis=-1) * scale
    return y * jax.nn.silu(z)


def mlstm_scan(q, k, v, logi, logf):
    B, H, S, dk = q.shape
    dv = v.shape[-1]
    L = MLSTM_CHUNK
    NC = S // L

    def chunks(a):
        return jnp.moveaxis(a.reshape(B, H, NC, L, *a.shape[3:]), 2, 0)

    mask = jnp.tril(jnp.ones((L, L), dtype=bool))

    def body(carry, inp):
        C, n, m = carry
        qc, kc, vc, li, lf = inp
        b = jnp.cumsum(lf, axis=-1)
        D = b[..., :, None] - b[..., None, :] + li[..., None, :]
        D = jnp.where(mask, D, -jnp.inf)
        m_inter = b + m[..., None]
        m_row = jnp.maximum(m_inter, jnp.max(D, axis=-1))
        w_intra = jnp.exp(D - m_row[..., None])
        w_inter = jnp.exp(m_inter - m_row)
        s = jnp.einsum('bhid,bhjd->bhij', qc, kc) * w_intra
        num = (jnp.einsum('bhij,bhje->bhie', s, vc)
               + w_inter[..., None] * jnp.einsum('bhid,bhde->bhie', qc, C))
        den = jnp.sum(s, axis=-1) + w_inter * jnp.einsum('bhid,bhd->bhi', qc, n)
        h = num / jnp.maximum(jnp.abs(den), jnp.exp(-m_row))[..., None]
        bL = b[..., -1]
        w_r = bL[..., None] - b + li
        m_new = jnp.maximum(bL + m, jnp.max(w_r, axis=-1))
        decay = jnp.exp(bL + m - m_new)
        wk = jnp.exp(w_r - m_new[..., None])
        C_new = decay[..., None, None] * C + jnp.einsum('bhj,bhjd,bhje->bhde', wk, kc, vc)
        n_new = decay[..., None] * n + jnp.einsum('bhj,bhjd->bhd', wk, kc)
        return (C_new, n_new, m_new), h

    init = (jnp.zeros((B, H, dk, dv), jnp.float32),
            jnp.zeros((B, H, dk), jnp.float32),
            jnp.zeros((B, H), jnp.float32))
    _, hs = lax.scan(body, init, (chunks(q), chunks(k), chunks(v), chunks(logi), chunks(logf)))
    return jnp.moveaxis(hs, 0, 2).reshape(B, H, S, dv)


def mlstm_mixer(q, k, v, o, z, gates, b_gate, norm_g):
    B, S, _ = q.shape

    def heads(a):
        return a.astype(jnp.float32).reshape(B, S, MLSTM_HEADS, MLSTM_HEAD_DIM).transpose(0, 2, 1, 3)

    qh = heads(q) * (MLSTM_HEAD_DIM ** -0.5)
    kh = heads(k)
    vh = heads(v)
    gt = (gates.astype(jnp.float32) + b_gate.astype(jnp.float32)).transpose(0, 2, 1)
    i_f, f_f, i_b, f_b = jnp.split(gt, 4, axis=1)
    h_fwd = mlstm_scan(qh, kh, vh, i_f, jax.nn.log_sigmoid(f_f))
    flip = lambda a: jnp.flip(a, axis=2)
    h_bwd = flip(mlstm_scan(flip(qh), flip(kh), flip(vh), flip(i_b), flip(jax.nn.log_sigmoid(f_b))))
    h = (h_fwd + h_bwd).transpose(0, 2, 1, 3)
    h = h * lax.rsqrt(jnp.mean(h * h, axis=-1, keepdims=True) + EPS)
    h = h.reshape(B, S, MLSTM_WIDTH) * norm_g.astype(jnp.float32)
    h = h * jax.nn.sigmoid(o.astype(jnp.float32))
    return h.astype(q.dtype) * jax.nn.silu(z)


def rope_half(x, ang):
    c = jnp.cos(ang)[None, :, None, :]
    s = jnp.sin(ang)[None, :, None, :]
    xa, xb = jnp.split(x, 2, axis=-1)
    return jnp.concatenate([xa * c - xb * s, xb * c + xa * s], axis=-1)


def rope_2d(x, ang_r, ang_c):
    xr, xc = jnp.split(x, 2, axis=-1)
    return jnp.concatenate([rope_half(xr, ang_r), rope_half(xc, ang_c)], axis=-1)


def attn_mixer(q, k, v, z, q_norm, k_norm, rows):
    B, S, _ = q.shape
    qh = rmsnorm(q.reshape(B, S, ATTN_HEADS, ATTN_HEAD_DIM), q_norm).astype(jnp.float32)
    kh = rmsnorm(k.reshape(B, S, ATTN_KV_HEADS, ATTN_HEAD_DIM), k_norm).astype(jnp.float32)
    vh = v.reshape(B, S, ATTN_KV_HEADS, ATTN_HEAD_DIM).astype(jnp.float32)
    row_ids = jnp.repeat(jnp.arange(rows), GRID_W).astype(jnp.float32)
    col_ids = jnp.tile(jnp.arange(GRID_W), rows).astype(jnp.float32)
    nf = ROPE_AXIS_DIM // 2
    inv = ROPE_THETA ** (-jnp.arange(nf, dtype=jnp.float32) / nf)
    ang_r = row_ids[:, None] * inv
    ang_c = col_ids[:, None] * inv
    qh = rope_2d(qh, ang_r, ang_c)
    kh = rope_2d(kh, ang_r, ang_c)
    NB = S // Q_BLOCK
    qb = qh.reshape(B, NB, Q_BLOCK, ATTN_KV_HEADS, ATTN_GROUP, ATTN_HEAD_DIM).transpose(1, 0, 3, 4, 2, 5)
    kt = kh.transpose(0, 2, 1, 3)
    vt = vh.transpose(0, 2, 1, 3)
    scale = ATTN_HEAD_DIM ** -0.5

    def block(qblk):
        s = jnp.einsum('bkgqd,bksd->bkgqs', qblk, kt) * scale
        p = jax.nn.softmax(s, axis=-1)
        return jnp.einsum('bkgqs,bksd->bkgqd', p, vt)

    ob = lax.map(block, qb)
    o = ob.transpose(1, 0, 4, 2, 3, 5).reshape(B, S, ATTN_WIDTH)
    return o.astype(q.dtype) * jax.nn.silu(z)


def layer(x, p, rows, norm_pre, w_in, b_gate, w_pool, pool_scale, mlstm_norm,
          q_norm, k_norm, w_out, norm_post, w_ple_proj, w_ple_gate, ple_norm):
    h = rmsnorm(x, norm_pre)
    zin = jnp.einsum('bsd,de->bse', h, w_in)
    offs = np.cumsum(IN_SPLITS)[:-1].tolist()
    (pu, pz, mq, mk, mv, mo, mz, mg, aq, ak, av, az) = jnp.split(zin, offs, axis=-1)
    y_pool = pool_mixer(pu, pz, w_pool, pool_scale)
    y_ml = mlstm_mixer(mq, mk, mv, mo, mz, mg, b_gate, mlstm_norm)
    y_att = attn_mixer(aq, ak, av, az, q_norm, k_norm, rows)
    y = jnp.concatenate([y_pool, y_ml, y_att], axis=-1)
    y = jnp.einsum('bsc,cd->bsd', y, w_out)
    x = x + rmsnorm(y, norm_post)
    ple = jnp.einsum('bsr,rd->bsd', p, w_ple_proj) * jax.nn.sigmoid(jnp.einsum('bsd,de->bse', x, w_ple_gate))
    return x + rmsnorm(ple, ple_norm)


def trunk(x, p, norm_pre, w_in, b_gate, w_pool, pool_scale, mlstm_norm,
          q_norm, k_norm, w_out, norm_post, w_ple_proj, w_ple_gate, ple_norm):
    rows = x.shape[1] // GRID_W
    for i in range(DEPTH):
        x = layer(x, p[i], rows, norm_pre[i], w_in[i], b_gate[i], w_pool[i], pool_scale[i],
                  mlstm_norm[i], q_norm[i], k_norm[i], w_out[i], norm_post[i],
                  w_ple_proj[i], w_ple_gate[i], ple_norm[i])
    return x


def setup_inputs(seed: int = 0) -> dict:
    key = jax.random.key(seed)
    ks = jax.random.split(key, 20)
    f32 = jnp.float32
    nrm = lambda k, shape: jax.random.normal(k, shape, f32)
    x_prompt = nrm(ks[0], (BATCH, SEQ, D_MODEL))
    x_sample = nrm(ks[1], (DEC_BATCH, DEC_SEQ, D_MODEL))
    p_prompt = nrm(ks[2], (DEPTH, BATCH, SEQ, PLE_DIM))
    p_sample = nrm(ks[3], (DEPTH, DEC_BATCH, DEC_SEQ, PLE_DIM))
    norm_pre = 1.0 + 0.1 * nrm(ks[4], (DEPTH, D_MODEL))
    w_in = nrm(ks[5], (DEPTH, D_MODEL, IN_WIDTH)) * (D_MODEL ** -0.5)
    i_base = jnp.full((MLSTM_HEADS,), -1.0, f32)
    f_base = jnp.linspace(3.0, 6.0, MLSTM_HEADS, dtype=f32)
    gate_base = jnp.concatenate([i_base, f_base, i_base, f_base])
    b_gate = gate_base[None, :] + 0.1 * nrm(ks[6], (DEPTH, N_GATES))
    w_pool = nrm(ks[7], (DEPTH, len(POOL_WINDOWS), POOL_GROUP, POOL_GROUP)) * (POOL_GROUP ** -0.5)
    pool_scale = 1.0 + 0.1 * nrm(ks[8], (DEPTH, POOL_WIDTH))
    mlstm_norm = 1.0 + 0.1 * nrm(ks[9], (DEPTH, MLSTM_WIDTH))
    q_norm = 1.0 + 0.1 * nrm(ks[10], (DEPTH, ATTN_HEAD_DIM))
    k_norm = 1.0 + 0.1 * nrm(ks[11], (DEPTH, ATTN_HEAD_DIM))
    w_out = nrm(ks[12], (DEPTH, MIX_WIDTH, D_MODEL)) * (MIX_WIDTH ** -0.5)
    norm_post = 1.0 + 0.1 * nrm(ks[13], (DEPTH, D_MODEL))
    w_ple_proj = nrm(ks[14], (DEPTH, PLE_DIM, D_MODEL)) * (PLE_DIM ** -0.5)
    w_ple_gate = nrm(ks[15], (DEPTH, D_MODEL, D_MODEL)) * (D_MODEL ** -0.5)
    ple_norm = 1.0 + 0.1 * nrm(ks[16], (DEPTH, D_MODEL))
    return {'x_prompt': x_prompt, 'x_sample': x_sample, 'p_prompt': p_prompt, 'p_sample': p_sample,
            'norm_pre': norm_pre, 'w_in': w_in, 'b_gate': b_gate, 'w_pool': w_pool,
            'pool_scale': pool_scale, 'mlstm_norm': mlstm_norm, 'q_norm': q_norm, 'k_norm': k_norm,
            'w_out': w_out, 'norm_post': norm_post, 'w_ple_proj': w_ple_proj,
            'w_ple_gate': w_ple_gate, 'ple_norm': ple_norm}


def reference(x_prompt, x_sample, p_prompt, p_sample, norm_pre, w_in, b_gate, w_pool,
              pool_scale, mlstm_norm, q_norm, k_norm, w_out, norm_post, w_ple_proj,
              w_ple_gate, ple_norm):
    y_prompt = trunk(x_prompt, p_prompt, norm_pre, w_in, b_gate, w_pool, pool_scale, mlstm_norm,
                     q_norm, k_norm, w_out, norm_post, w_ple_proj, w_ple_gate, ple_norm)
    y_sample = trunk(x_sample, p_sample, norm_pre, w_in, b_gate, w_pool, pool_scale, mlstm_norm,
                     q_norm, k_norm, w_out, norm_post, w_ple_proj, w_ple_gate, ple_norm)
    return (y_prompt, y_sample)
```

```python
import functools

import jax
import jax.numpy as jnp
from jax import lax
from jax.experimental import pallas as pl
from jax.experimental.pallas import tpu as pltpu

F32 = jnp.float32
BF16 = jnp.bfloat16

GRID_W = 64
POOL_WINDOWS = (2, 4, 8, 16)
POOL_HALO = 8
MLSTM_HEADS = 4
MLSTM_BLOCK = 256
ATTN_HEAD_DIM = 128
ATTN_KV_HEADS = 4
ROPE_THETA = 10000.0
EPS = 1e-6
VMEM_LIMIT_BYTES = 56 * 1024 * 1024


def _params(*sem):
    return pltpu.CompilerParams(dimension_semantics=sem, vmem_limit_bytes=VMEM_LIMIT_BYTES)


def _silu(z):
    return z * jax.nn.sigmoid(z)


def _norm_cast_kernel(x_ref, g_ref, o_ref):
    x = x_ref[...]
    r = lax.rsqrt(jnp.mean(x * x, axis=-1, keepdims=True) + EPS)
    o_ref[...] = (x * r * g_ref[...]).astype(o_ref.dtype)


def norm_cast(x, g, tm=256):
    t, d = x.shape
    return pl.pallas_call(
        _norm_cast_kernel,
        grid=(t // tm,),
        in_specs=[pl.BlockSpec((tm, d), lambda i: (i, 0)),
                  pl.BlockSpec((1, d), lambda i: (0, 0))],
        out_specs=pl.BlockSpec((tm, d), lambda i: (i, 0)),
        out_shape=jax.ShapeDtypeStruct((t, d), BF16),
        compiler_params=_params("parallel"),
        name="norm_cast",
    )(x, g.reshape(1, d))


def _residual_norm_kernel(x_ref, y_ref, g_ref, o_ref, ob_ref=None):
    y = y_ref[...]
    r = lax.rsqrt(jnp.mean(y * y, axis=-1, keepdims=True) + EPS)
    out = x_ref[...] + y * r * g_ref[...]
    o_ref[...] = out
    if ob_ref is not None:
        ob_ref[...] = out.astype(ob_ref.dtype)


def residual_norm(x, y, g, with_bf16, tm=256):
    t, d = x.shape
    row = pl.BlockSpec((tm, d), lambda i: (i, 0))
    out_shape = [jax.ShapeDtypeStruct((t, d), F32)]
    out_specs = [row]
    if with_bf16:
        out_shape.append(jax.ShapeDtypeStruct((t, d), BF16))
        out_specs.append(row)
    res = pl.pallas_call(
        _residual_norm_kernel,
        grid=(t // tm,),
        in_specs=[row, row, pl.BlockSpec((1, d), lambda i: (0, 0))],
        out_specs=out_specs,
        out_shape=out_shape,
        compiler_params=_params("parallel"),
        name="residual_norm",
    )(x, y, g.reshape(1, d))
    return res if with_bf16 else res[0]


def _mm_kernel(a_ref, w_ref, o_ref):
    o_ref[...] = jnp.dot(a_ref[...], w_ref[...], preferred_element_type=F32).astype(o_ref.dtype)


def matmul(a, w, out_dtype=F32, tm=1024, tn=1024):
    m, k = a.shape
    n = w.shape[1]
    tm, tn = min(tm, m), min(tn, n)
    return pl.pallas_call(
        _mm_kernel,
        grid=(m // tm, n // tn),
        in_specs=[pl.BlockSpec((tm, k), lambda i, j: (i, 0)),
                  pl.BlockSpec((k, tn), lambda i, j: (0, j))],
        out_specs=pl.BlockSpec((tm, tn), lambda i, j: (i, j)),
        out_shape=jax.ShapeDtypeStruct((m, n), out_dtype),
        compiler_params=_params("parallel", "arbitrary"),
        name="in_proj",
    )(a, w)


def _gates_kernel(h_ref, wg_ref, wgt_ref, bcol_ref, brow_ref, gcol_ref, grow_ref):
    h = h_ref[...]

    def finish(g, is_forget):
        ls = jnp.minimum(g, 0.0) - jnp.log1p(jnp.exp(-jnp.abs(g)))
        return jnp.where(is_forget, ls, g)

    gcol = jnp.dot(h, wg_ref[...], preferred_element_type=F32) + bcol_ref[...]
    lane = lax.broadcasted_iota(jnp.int32, gcol.shape, 1)
    gcol_ref[...] = finish(gcol, (lane // MLSTM_HEADS) % 2 == 1)
    grow = lax.dot_general(wgt_ref[...], h, (((1,), (1,)), ((), ())), preferred_element_type=F32) + brow_ref[...]
    sub = lax.broadcasted_iota(jnp.int32, grow.shape, 0)
    grow_ref[...] = finish(grow, (sub // MLSTM_HEADS) % 2 == 1)


def gates(h, wg, wgt, b_gate, tm=512):
    t, d = h.shape
    ng = wgt.shape[0]
    lanes = wg.shape[1]
    bcol = jnp.zeros((1, lanes), F32).at[0, :ng].set(b_gate)
    brow = b_gate.reshape(ng, 1)
    return pl.pallas_call(
        _gates_kernel,
        grid=(t // tm,),
        in_specs=[pl.BlockSpec((tm, d), lambda i: (i, 0)),
                  pl.BlockSpec((d, lanes), lambda i: (0, 0)),
                  pl.BlockSpec((ng, d), lambda i: (0, 0)),
                  pl.BlockSpec((1, lanes), lambda i: (0, 0)),
                  pl.BlockSpec((ng, 1), lambda i: (0, 0))],
        out_specs=[pl.BlockSpec((tm, lanes), lambda i: (i, 0)),
                   pl.BlockSpec((ng, tm), lambda i: (0, i))],
        out_shape=[jax.ShapeDtypeStruct((t, lanes), F32),
                   jax.ShapeDtypeStruct((ng, t), F32)],
        compiler_params=_params("parallel"),
        name="gates",
    )(h, wg, wgt, bcol, brow)


def _out_proj_kernel(a1_ref, a2_ref, a3_ref, w1_ref, w2_ref, w3_ref, o_ref):
    acc = jnp.dot(a1_ref[...], w1_ref[...], preferred_element_type=F32)
    acc += jnp.dot(a2_ref[...], w2_ref[...], preferred_element_type=F32)
    acc += jnp.dot(a3_ref[...], w3_ref[...], preferred_element_type=F32)
    o_ref[...] = acc


def out_proj(y_pool, y_ml, y_att, w_out, tm=1024, tn=1024):
    m = y_pool.shape[0]
    k1, k2, k3 = y_pool.shape[1], y_ml.shape[1], y_att.shape[1]
    assert k1 == k2 and k3 == k1 + k2 and w_out.shape[0] == k1 + k2 + k3
    n = w_out.shape[1]
    tm, tn = min(tm, m), min(tn, n)
    return pl.pallas_call(
        _out_proj_kernel,
        grid=(m // tm, n // tn),
        in_specs=[pl.BlockSpec((tm, k1), lambda i, j: (i, 0)),
                  pl.BlockSpec((tm, k2), lambda i, j: (i, 0)),
                  pl.BlockSpec((tm, k3), lambda i, j: (i, 0)),
                  pl.BlockSpec((k1, tn), lambda i, j: (0, j)),
                  pl.BlockSpec((k2, tn), lambda i, j: (1, j)),
                  pl.BlockSpec((k3, tn), lambda i, j: (1, j))],
        out_specs=pl.BlockSpec((tm, tn), lambda i, j: (i, j)),
        out_shape=jax.ShapeDtypeStruct((m, n), F32),
        compiler_params=_params("parallel", "arbitrary"),
        name="out_proj",
    )(y_pool, y_ml, y_att, w_out, w_out, w_out)


def _ple_kernel(p_ref, x_ref, wp_ref, wg_ref, o_ref):
    proj = jnp.dot(p_ref[...].astype(BF16), wp_ref[...], preferred_element_type=F32)
    gate = jnp.dot(x_ref[...], wg_ref[...], preferred_element_type=F32)
    o_ref[...] = proj * jax.nn.sigmoid(gate)


def ple(p, xb, w_proj, w_gate, tm=1024, tn=1024):
    m, r = p.shape
    d = xb.shape[1]
    n = w_gate.shape[1]
    tm, tn = min(tm, m), min(tn, n)
    return pl.pallas_call(
        _ple_kernel,
        grid=(m // tm, n // tn),
        in_specs=[pl.BlockSpec((tm, r), lambda i, j: (i, 0)),
                  pl.BlockSpec((tm, d), lambda i, j: (i, 0)),
                  pl.BlockSpec((r, tn), lambda i, j: (0, j)),
                  pl.BlockSpec((d, tn), lambda i, j: (0, j))],
        out_specs=pl.BlockSpec((tm, tn), lambda i, j: (i, j)),
        out_shape=jax.ShapeDtypeStruct((m, n), F32),
        compiler_params=_params("parallel", "arbitrary"),
        name="ple",
    )(p, xb, w_proj, w_gate)


def _pool_kernel(u_ref, up_ref, un_ref, z_ref, w_ref, sc_ref, o_ref, *, seq_len):
    tm, width = u_ref.shape
    group = width // len(POOL_WINDOWS)
    i = pl.program_id(1)
    last = pl.num_programs(1) - 1
    u = u_ref[...]
    prev = jnp.where(i > 0, up_ref[...], 0.0)
    nxt = jnp.where(i < last, un_ref[...], 0.0)
    ext = jnp.concatenate([prev, u, nxt], axis=0)
    n_ext = tm + 2 * POOL_HALO
    t = i * tm + lax.broadcasted_iota(jnp.int32, (tm, 1), 0)
    outs = []
    for g, win in enumerate(POOL_WINDOWS):
        s = ext[:, g * group:(g + 1) * group]
        k = 1
        while k < win:
            s = s + pltpu.roll(s, n_ext - k, axis=0)
            k *= 2
        start = POOL_HALO - win // 2
        wsum = s[start:start + tm]
        lo = jnp.maximum(t - win // 2, 0)
        hi = jnp.minimum(t + win // 2 - 1, seq_len - 1)
        cnt = (hi - lo + 1).astype(F32)
        d = (wsum / cnt - u[:, g * group:(g + 1) * group]).astype(BF16)
        outs.append(jnp.dot(d, w_ref[g], preferred_element_type=F32))
    y = jnp.concatenate(outs, axis=-1) * sc_ref[...]
    o_ref[...] = (y * _silu(z_ref[...])).astype(o_ref.dtype)


def pool_mixer(zin, u_blk, z_blk, width, w_pool, scale, batch, seq_len, tm=256):
    nt = seq_len // tm
    hb = tm // POOL_HALO
    n_halo = batch * seq_len // POOL_HALO
    kern = functools.partial(_pool_kernel, seq_len=seq_len)
    return pl.pallas_call(
        kern,
        grid=(batch, nt),
        in_specs=[pl.BlockSpec((tm, width), lambda b, i: (b * nt + i, u_blk)),
                  pl.BlockSpec((POOL_HALO, width), lambda b, i: (jnp.maximum((b * nt + i) * hb - 1, 0), u_blk)),
                  pl.BlockSpec((POOL_HALO, width), lambda b, i: (jnp.minimum((b * nt + i + 1) * hb, n_halo - 1), u_blk)),
                  pl.BlockSpec((tm, width), lambda b, i: (b * nt + i, z_blk)),
                  pl.BlockSpec(w_pool.shape, lambda b, i: (0, 0, 0)),
                  pl.BlockSpec((1, width), lambda b, i: (0, 0))],
        out_specs=pl.BlockSpec((tm, width), lambda b, i: (b * nt + i, 0)),
        out_shape=jax.ShapeDtypeStruct((batch * seq_len, width), BF16),
        compiler_params=_params("parallel", "parallel"),
        name="pool_mixer",
    )(zin, zin, zin, zin, w_pool, scale.reshape(1, width))


def _mlstm_kernel(qf_ref, kf_ref, vf_ref, qb_ref, kb_ref, vb_ref,
                  gcf_ref, gcb_ref, grf_ref, grb_ref,
                  hf_ref, hb_ref, c_ref, n_ref, m_ref):
    blk = qf_ref.shape[0]
    hd = qf_ref.shape[1] // MLSTM_HEADS
    scale = hd ** -0.5

    @pl.when(pl.program_id(1) == 0)
    def _():
        c_ref[...] = jnp.zeros_like(c_ref)
        n_ref[...] = jnp.zeros_like(n_ref)
        m_ref[...] = jnp.zeros_like(m_ref)

    row = lax.broadcasted_iota(jnp.int32, (blk, blk), 0)
    col = lax.broadcasted_iota(jnp.int32, (blk, blk), 1)
    tril = row >= col
    hi = lax.Precision.HIGHEST

    for d, (q_ref, k_ref, v_ref, gc_ref, gr_ref, h_ref) in enumerate((
            (qf_ref, kf_ref, vf_ref, gcf_ref, grf_ref, hf_ref),
            (qb_ref, kb_ref, vb_ref, gcb_ref, grb_ref, hb_ref))):
        mask = tril if d == 0 else jnp.logical_not(row > col)
        ones = mask.astype(F32)
        gc = gc_ref[...]
        gr = gr_ref[...]
        cum_c = jnp.dot(ones, gc, precision=hi, preferred_element_type=F32)
        cum_r = lax.dot_general(gr, ones, (((1,), (1,)), ((), ())), precision=hi, preferred_element_type=F32)
        end = blk - 1 if d == 0 else 0
        for h in range(MLSTM_HEADS):
            gi = d * 2 * MLSTM_HEADS + h
            gf = gi + MLSTM_HEADS
            sidx = d * MLSTM_HEADS + h
            li_c = gc[:, gi:gi + 1]
            li_r = gr[gi:gi + 1, :]
            b_c = cum_c[:, gf:gf + 1]
            b_r = cum_r[gf:gf + 1, :]
            b_end = cum_r[gf:gf + 1, end:end + 1]
            m_prev = m_ref[sidx:sidx + 1, 0:1]

            dmat = jnp.where(mask, b_c - b_r + li_r, -jnp.inf)
            m_inter = b_c + m_prev
            m_row = jnp.maximum(m_inter, jnp.max(dmat, axis=-1, keepdims=True))
            w_intra = jnp.exp(dmat - m_row)
            w_inter = jnp.exp(m_inter - m_row)

            sl = slice(h * hd, (h + 1) * hd)
            q = q_ref[:, sl] * scale
            k = k_ref[:, sl]
            qb16 = q.astype(BF16)
            v16 = v_ref[:, sl].astype(BF16)
            c_old = c_ref[sidx]
            n_old = n_ref[sidx:sidx + 1, :]

            s = lax.dot_general(qb16, k.astype(BF16), (((1,), (1,)), ((), ())),
                                preferred_element_type=F32) * w_intra
            num = (jnp.dot(s.astype(BF16), v16, preferred_element_type=F32)
                   + w_inter * jnp.dot(qb16, c_old.astype(BF16), preferred_element_type=F32))
            den = (jnp.sum(s, axis=-1, keepdims=True)
                   + w_inter * jnp.sum(q * n_old, axis=-1, keepdims=True))
            inv = 1.0 / jnp.maximum(jnp.abs(den), jnp.exp(-m_row))
            h_ref[:, sl] = num * inv

            w_r = b_end - b_r + li_r
            m_new = jnp.maximum(b_end + m_prev, jnp.max(w_r, axis=-1, keepdims=True))
            decay = jnp.exp(b_end + m_prev - m_new)
            kw = k * jnp.exp(b_end - b_c + li_c - m_new)
            upd = lax.dot_general(kw.astype(BF16), v16, (((0,), (0,)), ((), ())),
                                  preferred_element_type=F32)
            c_ref[sidx] = decay * c_old + upd
            n_ref[sidx:sidx + 1, :] = decay * n_old + jnp.sum(kw, axis=0, keepdims=True)
            m_ref[sidx:sidx + 1, :] = jnp.broadcast_to(m_new, (1, m_ref.shape[1]))


def mlstm_scan(zin, q_blk, k_blk, v_blk, width, gcol, grow, batch, seq_len):
    blk = min(MLSTM_BLOCK, seq_len)
    nb = seq_len // blk
    hd = width // MLSTM_HEADS
    lanes = gcol.shape[1]
    ng = grow.shape[0]

    def fwd(c):
        return lambda b, i: (b * nb + i, c)

    def bwd(c):
        return lambda b, i: (b * nb + nb - 1 - i, c)

    def fwd_t(b, i):
        return (0, b * nb + i)

    def bwd_t(b, i):
        return (0, b * nb + nb - 1 - i)

    big = lambda f, c: pl.BlockSpec((blk, width), f(c))
    t = batch * seq_len
    return pl.pallas_call(
        _mlstm_kernel,
        grid=(batch, nb),
        in_specs=[big(fwd, q_blk), big(fwd, k_blk), big(fwd, v_blk),
                  big(bwd, q_blk), big(bwd, k_blk), big(bwd, v_blk),
                  pl.BlockSpec((blk, lanes), fwd(0)), pl.BlockSpec((blk, lanes), bwd(0)),
                  pl.BlockSpec((ng, blk), fwd_t), pl.BlockSpec((ng, blk), bwd_t)],
        out_specs=[pl.BlockSpec((blk, width), fwd(0)), pl.BlockSpec((blk, width), bwd(0))],
        out_shape=[jax.ShapeDtypeStruct((t, width), F32), jax.ShapeDtypeStruct((t, width), F32)],
        scratch_shapes=[pltpu.VMEM((2 * MLSTM_HEADS, hd, hd), F32),
                        pltpu.VMEM((2 * MLSTM_HEADS, hd), F32),
                        pltpu.VMEM((2 * MLSTM_HEADS, 128), F32)],
        compiler_params=_params("parallel", "arbitrary"),
        name="mlstm_scan",
    )(zin, zin, zin, zin, zin, zin, gcol, gcol, grow, grow)


def _mlstm_out_kernel(hf_ref, hb_ref, o_ref, z_ref, g_ref, y_ref):
    width = hf_ref.shape[1]
    hd = width // MLSTM_HEADS
    h = hf_ref[...] + hb_ref[...]
    parts = []
    for i in range(MLSTM_HEADS):
        hh = h[:, i * hd:(i + 1) * hd]
        parts.append(hh * lax.rsqrt(jnp.mean(hh * hh, axis=-1, keepdims=True) + EPS))
    hn = jnp.concatenate(parts, axis=-1) * g_ref[...]
    hn = hn * jax.nn.sigmoid(o_ref[...])
    y_ref[...] = (hn * _silu(z_ref[...])).astype(y_ref.dtype)


def mlstm_out(hf, hb, zin, o_blk, z_blk, norm_g, tm=512):
    t, width = hf.shape
    tm = min(tm, t)
    row = pl.BlockSpec((tm, width), lambda i: (i, 0))
    return pl.pallas_call(
        _mlstm_out_kernel,
        grid=(t // tm,),
        in_specs=[row, row,
                  pl.BlockSpec((tm, width), lambda i: (i, o_blk)),
                  pl.BlockSpec((tm, width), lambda i: (i, z_blk)),
                  pl.BlockSpec((1, width), lambda i: (0, 0))],
        out_specs=row,
        out_shape=jax.ShapeDtypeStruct((t, width), BF16),
        compiler_params=_params("parallel"),
        name="mlstm_out",
    )(hf, hb, zin, zin, norm_g.reshape(1, width))


def rope_tables(seq_len):
    rows = seq_len // GRID_W
    row_ids = jnp.repeat(jnp.arange(rows), GRID_W).astype(F32)
    col_ids = jnp.tile(jnp.arange(GRID_W), rows).astype(F32)
    nf = ATTN_HEAD_DIM // 4
    inv = ROPE_THETA ** (-jnp.arange(nf, dtype=F32) / nf)
    ang_r = row_ids[:, None] * inv
    ang_c = col_ids[:, None] * inv
    cos = jnp.concatenate([jnp.cos(ang_r)] * 2 + [jnp.cos(ang_c)] * 2, axis=-1)
    sin = jnp.concatenate([-jnp.sin(ang_r), jnp.sin(ang_r), -jnp.sin(ang_c), jnp.sin(ang_c)], axis=-1)
    return cos, sin


def _attn_prep_kernel(q_ref, k_ref, v_ref, cos_ref, sin_ref, qn_ref, kn_ref, qo_ref, ko_ref, vo_ref):
    hd = ATTN_HEAD_DIM
    cos = cos_ref[...]
    sin = sin_ref[...]
    lane = lax.broadcasted_iota(jnp.int32, cos.shape, 1)
    first = (lane % (hd // 2)) < (hd // 4)

    def norm_rope(x, g):
        x = x * lax.rsqrt(jnp.mean(x * x, axis=-1, keepdims=True) + EPS) * g
        swapped = jnp.where(first, pltpu.roll(x, hd - hd // 4, axis=1), pltpu.roll(x, hd // 4, axis=1))
        return x * cos + swapped * sin

    scale = hd ** -0.5
    for h in range(q_ref.shape[1] // hd):
        sl = slice(h * hd, (h + 1) * hd)
        qo_ref[:, sl] = (norm_rope(q_ref[:, sl], qn_ref[...]) * scale).astype(qo_ref.dtype)
    for h in range(k_ref.shape[1] // hd):
        sl = slice(h * hd, (h + 1) * hd)
        ko_ref[:, sl] = norm_rope(k_ref[:, sl], kn_ref[...]).astype(ko_ref.dtype)
    vo_ref[...] = v_ref[...].astype(vo_ref.dtype)


def attn_prep(zin, q_blk, k_blk, v_blk, q_width, kv_width, q_norm, k_norm, cos, sin, batch, seq_len, tm=256):
    t = batch * seq_len
    nt = seq_len // tm
    hd = ATTN_HEAD_DIM
    pos = pl.BlockSpec((tm, hd), lambda i: (i % nt, 0))
    vec = pl.BlockSpec((1, hd), lambda i: (0, 0))
    return pl.pallas_call(
        _attn_prep_kernel,
        grid=(t // tm,),
        in_specs=[pl.BlockSpec((tm, q_width), lambda i: (i, q_blk)),
                  pl.BlockSpec((tm, kv_width), lambda i: (i, k_blk)),
                  pl.BlockSpec((tm, kv_width), lambda i: (i, v_blk)),
                  pos, pos, vec, vec],
        out_specs=[pl.BlockSpec((tm, q_width), lambda i: (i, 0)),
                   pl.BlockSpec((tm, kv_width), lambda i: (i, 0)),
                   pl.BlockSpec((tm, kv_width), lambda i: (i, 0))],
        out_shape=[jax.ShapeDtypeStruct((t, q_width), BF16),
                   jax.ShapeDtypeStruct((t, kv_width), BF16),
                   jax.ShapeDtypeStruct((t, kv_width), BF16)],
        compiler_params=_params("parallel"),
        name="attn_prep",
    )(zin, zin, zin, cos, sin, q_norm.reshape(1, hd), k_norm.reshape(1, hd))


def _flash_kernel(q_ref, k_ref, v_ref, z_ref, o_ref, *, tk):
    hd = ATTN_HEAD_DIM
    tq = q_ref.shape[0]
    group = q_ref.shape[1] // hd
    rows = group * tq
    q = jnp.concatenate([q_ref[:, g * hd:(g + 1) * hd] for g in range(group)], axis=0)
    nk = k_ref.shape[0] // tk

    def body(j, carry):
        m, l, acc = carry
        start = pl.multiple_of(j * tk, tk)
        kb = k_ref[pl.ds(start, tk), :]
        vb = v_ref[pl.ds(start, tk), :]
        s = lax.dot_general(q, kb, (((1,), (1,)), ((), ())), preferred_element_type=F32)
        m_new = jnp.maximum(m, jnp.max(s, axis=-1, keepdims=True))
        p = jnp.exp(s - m_new)
        alpha = jnp.exp(m - m_new)
        l = alpha * l + jnp.sum(p, axis=-1, keepdims=True)
        acc = alpha * acc + jnp.dot(p.astype(BF16), vb, preferred_element_type=F32)
        return m_new, l, acc

    init = (jnp.full((rows, 1), -jnp.inf, F32), jnp.zeros((rows, 1), F32), jnp.zeros((rows, hd), F32))
    _, l, acc = lax.fori_loop(0, nk, body, init)
    out = acc * (1.0 / l)
    out = jnp.concatenate([out[g * tq:(g + 1) * tq] for g in range(group)], axis=-1)
    o_ref[...] = (out * _silu(z_ref[...])).astype(o_ref.dtype)


def flash_attention(q, k, v, zin, z_blk0, batch, seq_len, tq=256, tk=512):
    hd = ATTN_HEAD_DIM
    kvh = k.shape[1] // hd
    gw = q.shape[1] // kvh
    tq, tk = min(tq, seq_len), min(tk, seq_len)
    nq = seq_len // tq
    kern = functools.partial(_flash_kernel, tk=tk)
    return pl.pallas_call(
        kern,
        grid=(batch, kvh, nq),
        in_specs=[pl.BlockSpec((tq, gw), lambda b, h, i: (b * nq + i, h)),
                  pl.BlockSpec((seq_len, hd), lambda b, h, i: (b, h)),
                  pl.BlockSpec((seq_len, hd), lambda b, h, i: (b, h)),
                  pl.BlockSpec((tq, gw), lambda b, h, i: (b * nq + i, z_blk0 + h))],
        out_specs=pl.BlockSpec((tq, gw), lambda b, h, i: (b * nq + i, h)),
        out_shape=jax.ShapeDtypeStruct(q.shape, BF16),
        compiler_params=_params("parallel", "parallel", "arbitrary"),
        name="flash_attention",
    )(q, k, v, zin)


def _split_w_in(w_in, d):
    pw = d // 4
    ng = 4 * MLSTM_HEADS
    aw = d - 2 * pw
    kvw = ATTN_KV_HEADS * ATTN_HEAD_DIM
    o_gate = 7 * pw
    o_aq = o_gate + ng
    o_ak = o_aq + aw
    o_av = o_ak + kvw
    o_az = o_av + kvw
    assert w_in.shape[1] == o_az + aw
    main = jnp.concatenate([w_in[:, o_aq:o_ak], w_in[:, o_az:], w_in[:, :o_gate],
                            w_in[:, o_ak:o_az]], axis=1).astype(BF16)
    wg = w_in[:, o_gate:o_aq]
    wg_col = jnp.zeros((d, 128), F32).at[:, :ng].set(wg).astype(BF16)
    wg_row = wg.T.astype(BF16)
    return main, wg_col, wg_row


def _layer(x, p, batch, seq_len, cos, sin, wl):
    d = x.shape[1]
    pw = d // 4
    aw = d - 2 * pw
    kvw = ATTN_KV_HEADS * ATTN_HEAD_DIM
    blk_pool = aw * 2 // pw
    blk_ml = blk_pool + 2
    blk_kv = (2 * aw + 7 * pw) // kvw
    blk_az = aw // (aw // ATTN_KV_HEADS)

    h = norm_cast(x, wl["norm_pre"])
    zin = matmul(h, wl["w_main"])
    gcol, grow = gates(h, wl["wg_col"], wl["wg_row"], wl["b_gate"])

    y_pool = pool_mixer(zin, blk_pool, blk_pool + 1, pw, wl["w_pool"], wl["pool_scale"], batch, seq_len)

    hf, hb = mlstm_scan(zin, blk_ml, blk_ml + 1, blk_ml + 2, pw, gcol, grow, batch, seq_len)
    y_ml = mlstm_out(hf, hb, zin, blk_ml + 3, blk_ml + 4, wl["mlstm_norm"])

    q, k, v = attn_prep(zin, 0, blk_kv, blk_kv + 1, aw, kvw, wl["q_norm"], wl["k_norm"], cos, sin, batch, seq_len)
    y_att = flash_attention(q, k, v, zin, blk_az, batch, seq_len)

    y = out_proj(y_pool, y_ml, y_att, wl["w_out"])
    x1, x1b = residual_norm(x, y, wl["norm_post"], with_bf16=True)
    e = ple(p, x1b, wl["w_ple_proj"], wl["w_ple_gate"])
    return residual_norm(x1, e, wl["ple_norm"], with_bf16=False)


def _trunk(x, p, layers):
    batch, seq_len, d = x.shape
    cos, sin = rope_tables(seq_len)
    xf = x.reshape(batch * seq_len, d)
    for i, wl in enumerate(layers):
        xf = _layer(xf, p[i].reshape(batch * seq_len, -1), batch, seq_len, cos, sin, wl)
    return xf.reshape(batch, seq_len, d)


def kernel(x_prompt, x_sample, p_prompt, p_sample, norm_pre, w_in, b_gate, w_pool, pool_scale, mlstm_norm,
           q_norm, k_norm, w_out, norm_post, w_ple_proj, w_ple_gate, ple_norm):
    d = x_prompt.shape[-1]
    layers = []
    for i in range(w_in.shape[0]):
        w_main, wg_col, wg_row = _split_w_in(w_in[i], d)
        layers.append(dict(
            norm_pre=norm_pre[i], w_main=w_main, wg_col=wg_col, wg_row=wg_row, b_gate=b_gate[i],
            w_pool=w_pool[i].astype(BF16), pool_scale=pool_scale[i], mlstm_norm=mlstm_norm[i],
            q_norm=q_norm[i], k_norm=k_norm[i], w_out=w_out[i].astype(BF16), norm_post=norm_post[i],
            w_ple_proj=w_ple_proj[i].astype(BF16), w_ple_gate=w_ple_gate[i].astype(BF16), ple_norm=ple_norm[i]))
    return (_trunk(x_prompt, p_prompt, layers), _trunk(x_sample, p_sample, layers))
```
